```python
import jax, jax.numpy as jnp
from jax import lax
import numpy as np

D_MODEL = 2048
BATCH = 4
SEQ = 2048
DEPTH = 4
DEC_BATCH = 128
DEC_SEQ = 4
PAST_LEN = 16384
PAGE_SIZE = 128

N_META = 16
D_A = D_MODEL // 2
D_B = D_MODEL // 2
K_A = 31
K_B = 3
D_IN_EVEN = 2 * D_A + 3 * D_B
POOL_WINDOWS = (2, 4, 8, 16)
N_POOL_GROUPS = 4
D_POOL = D_MODEL // N_POOL_GROUPS
POOL_BUF = 15
D_FF = 4 * D_MODEL
N_EVEN = (DEPTH + 1) // 2
N_ODD = DEPTH // 2
EPS = 1e-6

kernel_name = "hybrid_conv_pool_decoder_step"


def rmsnorm(x, g):
    xf = x.astype(jnp.float32)
    y = xf * lax.rsqrt(jnp.mean(xf * xf, axis=-1, keepdims=True) + EPS)
    return (y * g.astype(jnp.float32)).astype(x.dtype)


def layernorm(x, g, b):
    xf = x.astype(jnp.float32)
    mu = jnp.mean(xf, axis=-1, keepdims=True)
    xc = xf - mu
    y = xc * lax.rsqrt(jnp.mean(xc * xc, axis=-1, keepdims=True) + EPS)
    return (y * g.astype(jnp.float32) + b.astype(jnp.float32)).astype(x.dtype)


def causal_dwconv(x_ext, w):
    c = x_ext.shape[-1]
    return lax.conv_general_dilated(
        x_ext, w[:, None, :].astype(x_ext.dtype), window_strides=(1,), padding='VALID',
        dimension_numbers=('NWC', 'WIO', 'NWC'), feature_group_count=c)


def even_mixer(h, buf_a, buf_b, w_in, conv_a_w, conv_a_b, ln_g, ln_b, conv_b_w, w_out):
    proj = jnp.einsum('nld,de->nle', h, w_in)
    a_val, a_gate, b_gate, c_gate, b_x = jnp.split(
        proj, [D_A, 2 * D_A, 2 * D_A + D_B, 2 * D_A + 2 * D_B], axis=-1)
    a = a_val * jax.nn.sigmoid(a_gate)
    a_ext = jnp.concatenate([buf_a, a], axis=1)
    a_conv = causal_dwconv(a_ext, conv_a_w) + conv_a_b.astype(a.dtype)
    a_out = jax.nn.silu(layernorm(a_conv, ln_g, ln_b))
    v = c_gate * b_x
    v_ext = jnp.concatenate([buf_b, v], axis=1)
    b_out = b_gate * causal_dwconv(v_ext, conv_b_w)
    y = jnp.einsum('nle,ed->nld', jnp.concatenate([a_out, b_out], axis=-1), w_out)
    return y, a_ext[:, -(K_A - 1):], v_ext[:, -(K_B - 1):]


def pool_mixer(h, buf, pos, w_groups, scale):
    l = h.shape[1]
    u_ext = jnp.concatenate([buf, h], axis=1)
    cs = jnp.cumsum(u_ext.astype(jnp.float32), axis=1)
    cs = jnp.pad(cs, ((0, 0), (1, 0), (0, 0)))
    end = cs[:, POOL_BUF + 1:]
    hf = h.astype(jnp.float32)
    outs = []
    for g, w in enumerate(POOL_WINDOWS):
        sl = slice(g * D_POOL, (g + 1) * D_POOL)
        start = cs[:, POOL_BUF + 1 - w:POOL_BUF + 1 - w + l, sl]
        cnt = jnp.minimum(pos + 1, w).astype(jnp.float32)[None, :, None]
        pooled = (end[..., sl] - start) / cnt - hf[..., sl]
        outs.append(jnp.einsum('nlc,cd->nld', pooled, w_groups[g].astype(jnp.float32)))
    y = jnp.concatenate(outs, axis=-1) * scale.astype(jnp.float32)
    return y.astype(h.dtype), u_ext[:, -POOL_BUF:]


def squared_relu_mlp(h, w_up, w_down):
    return jnp.einsum('nlf,fd->nld', jnp.square(jax.nn.relu(jnp.einsum('nld,df->nlf', h, w_up))), w_down)


def trunk(x, pos, bufs_a, bufs_b, bufs_p, norm_mix, norm_mlp, norm_final, w_in_even,
          conv_a_w, conv_a_b, ln_a_g, ln_a_b, conv_b_w, w_out_even, pool_w, pool_scale,
          w_mlp_up, w_mlp_down):
    new_a, new_b, new_p = [], [], []
    for layer in range(DEPTH):
        h = rmsnorm(x, norm_mix[layer])
        if layer % 2 == 0:
            e = layer // 2
            y, na, nb = even_mixer(h, bufs_a[e], bufs_b[e], w_in_even[e], conv_a_w[e], conv_a_b[e],
                                   ln_a_g[e], ln_a_b[e], conv_b_w[e], w_out_even[e])
            new_a.append(na)
            new_b.append(nb)
        else:
            o = layer // 2
            y, npool = pool_mixer(h, bufs_p[o], pos, pool_w[o], pool_scale[o])
            new_p.append(npool)
        x = x + y
        x = x + squared_relu_mlp(rmsnorm(x, norm_mlp[layer]), w_mlp_up[layer], w_mlp_down[layer])
    x = rmsnorm(x, norm_final)
    return x, jnp.stack(new_a), jnp.stack(new_b), jnp.stack(new_p)


def setup_inputs(seed: int = 0) -> dict:
    key = jax.random.key(seed)
    ks = jax.random.split(key, 20)

    def nrm(k, shape, scale):
        return jax.random.normal(k, shape, jnp.float32) * scale

    return {
        "x_prompt": nrm(ks[0], (BATCH, SEQ, D_MODEL), 1.0),
        "x_sample": nrm(ks[1], (DEC_BATCH, DEC_SEQ, D_MODEL), 1.0),
        "state_conv_a": nrm(ks[2], (N_EVEN, DEC_BATCH, K_A - 1, D_A), 0.5),
        "state_conv_b": nrm(ks[3], (N_EVEN, DEC_BATCH, K_B - 1, D_B), 0.5),
        "state_pool": nrm(ks[4], (N_ODD, DEC_BATCH, POOL_BUF, D_MODEL), 1.0),
        "meta_tokens": nrm(ks[5], (N_META, D_MODEL), 1.0),
        "norm_mix": 1.0 + nrm(ks[6], (DEPTH, D_MODEL), 0.02),
        "norm_mlp": 1.0 + nrm(ks[7], (DEPTH, D_MODEL), 0.02),
        "norm_final": 1.0 + nrm(ks[8], (D_MODEL,), 0.02),
        "w_in_even": nrm(ks[9], (N_EVEN, D_MODEL, D_IN_EVEN), D_MODEL ** -0.5),
        "conv_a_w": nrm(ks[10], (N_EVEN, K_A, D_A), K_A ** -0.5),
        "conv_a_b": nrm(ks[11], (N_EVEN, D_A), 0.02),
        "ln_a_g": 1.0 + nrm(ks[12], (N_EVEN, D_A), 0.02),
        "ln_a_b": nrm(ks[13], (N_EVEN, D_A), 0.02),
        "conv_b_w": nrm(ks[14], (N_EVEN, K_B, D_B), K_B ** -0.5),
        "w_out_even": nrm(ks[15], (N_EVEN, D_A + D_B, D_MODEL), (D_A + D_B) ** -0.5),
        "pool_w": nrm(ks[16], (N_ODD, N_POOL_GROUPS, D_POOL, D_POOL), D_POOL ** -0.5),
        "pool_scale": 1.0 + nrm(ks[17], (N_ODD, D_MODEL), 0.02),
        "w_mlp_up": nrm(ks[18], (DEPTH, D_MODEL, D_FF), D_MODEL ** -0.5),
        "w_mlp_down": nrm(ks[19], (DEPTH, D_FF, D_MODEL), D_FF ** -0.5),
    }


def reference(x_prompt, x_sample, state_conv_a, state_conv_b, state_pool, meta_tokens,
              norm_mix, norm_mlp, norm_final, w_in_even, conv_a_w, conv_a_b, ln_a_g, ln_a_b,
              conv_b_w, w_out_even, pool_w, pool_scale, w_mlp_up, w_mlp_down):
    params = (norm_mix, norm_mlp, norm_final, w_in_even, conv_a_w, conv_a_b, ln_a_g, ln_a_b,
              conv_b_w, w_out_even, pool_w, pool_scale, w_mlp_up, w_mlp_down)
    dt = x_prompt.dtype
    meta = jnp.broadcast_to(meta_tokens.astype(dt)[None], (BATCH, N_META, D_MODEL))
    xp = jnp.concatenate([meta, x_prompt], axis=1)
    pos_p = jnp.arange(N_META + SEQ, dtype=jnp.int32)
    zeros_a = [jnp.zeros((BATCH, K_A - 1, D_A), dt) for _ in range(N_EVEN)]
    zeros_b = [jnp.zeros((BATCH, K_B - 1, D_B), dt) for _ in range(N_EVEN)]
    zeros_p = [jnp.zeros((BATCH, POOL_BUF, D_MODEL), dt) for _ in range(N_ODD)]
    yp, new_conv_a_prompt, new_conv_b_prompt, new_pool_prompt = trunk(
        xp, pos_p, zeros_a, zeros_b, zeros_p, *params)
    y_prompt = yp[:, N_META:]
    pos_s = PAST_LEN + jnp.arange(DEC_SEQ, dtype=jnp.int32)
    bufs_a = [state_conv_a[e] for e in range(N_EVEN)]
    bufs_b = [state_conv_b[e] for e in range(N_EVEN)]
    bufs_p = [state_pool[o] for o in range(N_ODD)]
    y_sample, new_conv_a_sample, new_conv_b_sample, new_pool_sample = trunk(
        x_sample, pos_s, bufs_a, bufs_b, bufs_p, *params)
    return (y_prompt, y_sample, new_conv_a_prompt, new_conv_b_prompt, new_pool_prompt,
            new_conv_a_sample, new_conv_b_sample, new_pool_sample)
```

```python
import functools

import jax
import jax.numpy as jnp
from jax import lax
from jax.experimental import pallas as pl
from jax.experimental.pallas import tpu as pltpu

F32 = jnp.float32
BF16 = jnp.bfloat16
EPS = 1e-6
PAST_LEN = 16384
POOL_WINDOWS = (2, 4, 8, 16)
VMEM_LIMIT = 56 * 1024 * 1024
ROWS = 16
SUBLANES = 8


def _params(*sem):
    return pltpu.CompilerParams(dimension_semantics=sem, vmem_limit_bytes=VMEM_LIMIT)


def _rms(x, g):
    ms = jnp.mean(x * x, axis=-1, keepdims=True)
    return x * lax.rsqrt(ms + EPS) * g


def _bdot(a, w):
    return jnp.dot(a.astype(BF16), w.astype(BF16), preferred_element_type=F32)


def _inproj_kernel(x_ref, g_ref, w0, w1, w2, w3, w4, a_ref, v_ref, bg_ref, xn_ref):
    @pl.when(pl.program_id(1) == 0)
    def _():
        xn_ref[...] = _rms(x_ref[...], g_ref[...]).astype(BF16)

    xb = xn_ref[...]
    a_ref[...] = _bdot(xb, w0[...]) * jax.nn.sigmoid(_bdot(xb, w1[...]))
    bg_ref[...] = _bdot(xb, w2[...])
    v_ref[...] = _bdot(xb, w3[...]) * _bdot(xb, w4[...])


def _inproj(x, g, w_in, tm, tn=256):
    m, d = x.shape
    dc = w_in.shape[1] // 5
    nj = dc // tn
    wspec = lambda grp: pl.BlockSpec((d, tn), lambda i, j, grp=grp: (0, grp * nj + j))
    ospec = pl.BlockSpec((tm, tn), lambda i, j: (i, j))
    return pl.pallas_call(
        _inproj_kernel,
        grid=(m // tm, nj),
        in_specs=[pl.BlockSpec((tm, d), lambda i, j: (i, 0)),
                  pl.BlockSpec((1, d), lambda i, j: (0, 0))] + [wspec(k) for k in range(5)],
        out_specs=[ospec, ospec, ospec],
        out_shape=[jax.ShapeDtypeStruct((m, dc), F32)] * 3,
        scratch_shapes=[pltpu.VMEM((tm, d), BF16)],
        compiler_params=_params("parallel", "arbitrary"),
        name="inproj",
    )(x, g, w_in, w_in, w_in, w_in, w_in)


def _ln_silu(acc, g, b):
    mu = jnp.mean(acc, axis=-1, keepdims=True)
    xc = acc - mu
    var = jnp.mean(xc * xc, axis=-1, keepdims=True)
    y = xc * lax.rsqrt(var + EPS) * g + b
    return y * jax.nn.sigmoid(y)


def _evenmix_prompt_kernel(a_ref, v_ref, bg_ref, wa_ref, ba_ref, lg_ref, lb_ref, wb_ref,
                           u_ref, at_ref, vt_ref, aext, vext, ash, vsh, *, tt, ka, kb):
    tc = pl.program_id(1)
    ha = aext.shape[0] - tt
    hb = vext.shape[0] - tt
    dc = a_ref.shape[1]

    @pl.when(tc == 0)
    def _():
        aext[0:ha, :] = jnp.zeros((ha, dc), F32)
        vext[0:hb, :] = jnp.zeros((hb, dc), F32)

    @pl.when(tc > 0)
    def _():
        aext[0:ha, :] = aext[tt:tt + ha, :]
        vext[0:hb, :] = vext[tt:tt + hb, :]

    aext[ha:ha + tt, :] = a_ref[...]
    vext[hb:hb + tt, :] = v_ref[...]

    la = ash.shape[1]
    for r in range(1, SUBLANES):
        ash[r - 1] = aext[r:r + la, :]
    offs_b = [hb - kb + 1 + k for k in range(kb)]
    for s, o in enumerate(o for o in offs_b if o % SUBLANES):
        vsh[s] = vext[o:o + tt, :]

    def chunk(i, carry):
        base = pl.multiple_of(i * ROWS, ROWS)
        acc = jnp.broadcast_to(ba_ref[...], (ROWS, dc))
        for k in range(ka):
            q, r = divmod(ha - ka + 1 + k, SUBLANES)
            rows = pl.ds(base + q * SUBLANES, ROWS)
            src = aext[rows, :] if r == 0 else ash[r - 1, rows, :]
            acc = acc + wa_ref[k:k + 1, :] * src
        u_ref[pl.ds(base, ROWS), 0:dc] = _ln_silu(acc, lg_ref[...], lb_ref[...]).astype(BF16)
        vc, s = None, 0
        for k, o in enumerate(offs_b):
            if o % SUBLANES:
                src = vsh[s, pl.ds(base, ROWS), :]
                s += 1
            else:
                src = vext[pl.ds(base + o, ROWS), :]
            term = wb_ref[k:k + 1, :] * src
            vc = term if vc is None else vc + term
        u_ref[pl.ds(base, ROWS), dc:2 * dc] = (bg_ref[pl.ds(base, ROWS), :] * vc).astype(BF16)
        return carry

    lax.fori_loop(0, tt // ROWS, chunk, 0)

    @pl.when(tc == pl.num_programs(1) - 1)
    def _():
        at_ref[0] = aext[tt:tt + ha, :]
        vt_ref[0] = vext[tt:tt + hb, :]


def _evenmix_prompt(a, v, bg, wa, ba, lg, lb, wb, nseq, tt):
    m, dc = a.shape
    t = m // nseq
    nt = t // tt
    ka, kb = wa.shape[0], wb.shape[0]
    ha, hb = 32, 8
    row = pl.BlockSpec((tt, dc), lambda b, c: (b * nt + c, 0))
    vec = lambda r: pl.BlockSpec((r, dc), lambda b, c: (0, 0))
    return pl.pallas_call(
        functools.partial(_evenmix_prompt_kernel, tt=tt, ka=ka, kb=kb),
        grid=(nseq, nt),
        in_specs=[row, row, row, vec(ka), vec(1), vec(1), vec(1), vec(kb)],
        out_specs=[pl.BlockSpec((tt, 2 * dc), lambda b, c: (b * nt + c, 0)),
                   pl.BlockSpec((1, ha, dc), lambda b, c: (b, 0, 0)),
                   pl.BlockSpec((1, hb, dc), lambda b, c: (b, 0, 0))],
        out_shape=[jax.ShapeDtypeStruct((m, 2 * dc), BF16),
                   jax.ShapeDtypeStruct((nseq, ha, dc), F32),
                   jax.ShapeDtypeStruct((nseq, hb, dc), F32)],
        scratch_shapes=[pltpu.VMEM((ha + tt, dc), F32), pltpu.VMEM((hb + tt, dc), F32),
                        pltpu.VMEM((SUBLANES - 1, tt + ha - SUBLANES, dc), F32),
                        pltpu.VMEM((sum(1 for k in range(kb) if (hb - kb + 1 + k) % SUBLANES), tt, dc), F32)],
        compiler_params=_params("arbitrary", "arbitrary"),
        name="evenmix_prompt",
    )(a, v, bg, wa, ba, lg, lb, wb)


def _evenmix_sample_kernel(*refs, nseq, nt, ka, kb):
    la, lb_ = ka - 1, kb - 1
    a_ref, v_ref, bg_ref = refs[0:3]
    ha_refs = refs[3:3 + la]
    hb_refs = refs[3 + la:3 + la + lb_]
    wa_ref, ba_ref, lg_ref, lb_ref, wb_ref, u_ref = refs[3 + la + lb_:]
    dc = a_ref.shape[1]

    def chunk(c, carry):
        n0 = pl.multiple_of(c * ROWS, ROWS)
        for t in range(nt):
            acc = jnp.broadcast_to(ba_ref[...], (ROWS, dc))
            for k in range(ka):
                j = t + k
                if j < la:
                    src = ha_refs[j][pl.ds(n0, ROWS), :]
                else:
                    src = a_ref[pl.ds((j - la) * nseq + n0, ROWS), :]
                acc = acc + wa_ref[k:k + 1, :] * src
            rows = pl.ds(t * nseq + n0, ROWS)
            u_ref[rows, 0:dc] = _ln_silu(acc, lg_ref[...], lb_ref[...]).astype(BF16)
            vc = None
            for k in range(kb):
                j = t + k
                if j < lb_:
                    src = hb_refs[j][pl.ds(n0, ROWS), :]
                else:
                    src = v_ref[pl.ds((j - lb_) * nseq + n0, ROWS), :]
                term = wb_ref[k:k + 1, :] * src
                vc = term if vc is None else vc + term
            u_ref[rows, dc:2 * dc] = (bg_ref[rows, :] * vc).astype(BF16)
        return carry

    lax.fori_loop(0, nseq // ROWS, chunk, 0)


def _history_blocks(hist):
    nseq, rows, ch = hist.shape
    flat = hist.reshape(nseq, rows * ch)
    return [flat] * rows, [pl.BlockSpec((nseq, ch), lambda i, j=j: (0, j)) for j in range(rows)]


def _full(arr):
    return pl.BlockSpec(arr.shape, lambda i: (0,) * arr.ndim)


def _evenmix_sample(a, v, bg, hist_a, hist_b, wa, ba, lg, lb, wb, nseq):
    m, dc = a.shape
    nt = m // nseq
    ka, kb = wa.shape[0], wb.shape[0]
    small = (wa, ba, lg, lb, wb)
    ops_a, specs_a = _history_blocks(hist_a)
    ops_b, specs_b = _history_blocks(hist_b)
    return pl.pallas_call(
        functools.partial(_evenmix_sample_kernel, nseq=nseq, nt=nt, ka=ka, kb=kb),
        grid=(1,),
        in_specs=[_full(a), _full(v), _full(bg)] + specs_a + specs_b + [_full(z) for z in small],
        out_specs=pl.BlockSpec((m, 2 * dc), lambda i: (0, 0)),
        out_shape=jax.ShapeDtypeStruct((m, 2 * dc), BF16),
        compiler_params=_params("arbitrary"),
        name="evenmix_sample",
    )(a, v, bg, *ops_a, *ops_b, *small)


def _outproj_kernel(x_ref, u_ref, w_ref, o_ref):
    o_ref[...] = x_ref[...] + _bdot(u_ref[...], w_ref[...])


def _outproj(x, u, w, tm, tn=512):
    m, d = x.shape
    return pl.pallas_call(
        _outproj_kernel,
        grid=(m // tm, d // tn),
        in_specs=[pl.BlockSpec((tm, tn), lambda i, j: (i, j)),
                  pl.BlockSpec((tm, u.shape[1]), lambda i, j: (i, 0)),
                  pl.BlockSpec((w.shape[0], tn), lambda i, j: (0, j))],
        out_specs=pl.BlockSpec((tm, tn), lambda i, j: (i, j)),
        out_shape=jax.ShapeDtypeStruct((m, d), F32),
        compiler_params=_params("parallel", "arbitrary"),
        name="outproj",
    )(x, u, w)


def _poolpre_prompt_kernel(x_ref, g_ref, p_ref, ht_ref, hext, s2, s4, s8, *, tt):
    assert POOL_WINDOWS == (2, 4, 8, 16)
    tc = pl.program_id(1)
    hh = hext.shape[0] - tt
    end = hh + tt
    d = x_ref.shape[1]
    dg = d // len(POOL_WINDOWS)
    lo = SUBLANES

    @pl.when(tc == 0)
    def _():
        hext[0:hh, :] = jnp.zeros((hh, d), F32)

    @pl.when(tc > 0)
    def _():
        hext[0:hh, :] = hext[tt:tt + hh, :]

    hext[hh:end, :] = _rms(x_ref[...], g_ref[...])
    for stage in (s2, s4, s8):
        stage[0:lo, :] = jnp.zeros((lo, stage.shape[1]), F32)
    s2[lo:end, :] = hext[lo:end, dg:] + hext[lo - 1:end - 1, dg:]
    s4[lo:end, :] = s2[lo:end, dg:] + s2[lo - 2:end - 2, dg:]
    s8[lo:end, :] = s4[lo:end, dg:] + s4[lo - 4:end - 4, dg:]
    sums = (hext[hh:end, 0:dg] + hext[hh - 1:end - 1, 0:dg],
            s2[hh:end, 0:dg] + s2[hh - 2:end - 2, 0:dg],
            s4[hh:end, 0:dg] + s4[hh - 4:end - 4, 0:dg],
            s8[hh:end, 0:dg] + s8[hh - 8:end - 8, 0:dg])
    pos = tc * tt + lax.broadcasted_iota(jnp.int32, (tt, 1), 0)
    for gi, w in enumerate(POOL_WINDOWS):
        cols = slice(gi * dg, (gi + 1) * dg)
        cnt = jnp.minimum(pos + 1, w).astype(F32)
        p_ref[:, cols] = (sums[gi] / cnt - hext[hh:end, cols]).astype(BF16)

    @pl.when(tc == pl.num_programs(1) - 1)
    def _():
        ht_ref[0] = hext[tt:end, :]


def _poolpre_prompt(x, g, nseq, tt):
    m, d = x.shape
    nt = (m // nseq) // tt
    hh = 24
    dg = d // len(POOL_WINDOWS)
    return pl.pallas_call(
        functools.partial(_poolpre_prompt_kernel, tt=tt),
        grid=(nseq, nt),
        in_specs=[pl.BlockSpec((tt, d), lambda b, c: (b * nt + c, 0)),
                  pl.BlockSpec((1, d), lambda b, c: (0, 0))],
        out_specs=[pl.BlockSpec((tt, d), lambda b, c: (b * nt + c, 0)),
                   pl.BlockSpec((1, hh, d), lambda b, c: (b, 0, 0))],
        out_shape=[jax.ShapeDtypeStruct((m, d), BF16),
                   jax.ShapeDtypeStruct((nseq, hh, d), F32)],
        scratch_shapes=[pltpu.VMEM((hh + tt, d), F32), pltpu.VMEM((hh + tt, 3 * dg), F32),
                        pltpu.VMEM((hh + tt, 2 * dg), F32), pltpu.VMEM((hh + tt, dg), F32)],
        compiler_params=_params("arbitrary", "arbitrary"),
        name="poolpre_prompt",
    )(x, g)


def _poolpre_sample_kernel(*refs, nseq, nt, pos0):
    lp = POOL_WINDOWS[-1] - 1
    x_ref, g_ref = refs[0:2]
    hp_refs = refs[2:2 + lp]
    p_ref, hn_ref = refs[2 + lp:]
    d = x_ref.shape[1]
    dg = d // len(POOL_WINDOWS)
    hn_ref[...] = _rms(x_ref[...], g_ref[...])

    def chunk(c, carry):
        n0 = pl.multiple_of(c * ROWS, ROWS)
        for t in range(nt):
            for gi, w in enumerate(POOL_WINDOWS):
                cols = slice(gi * dg, (gi + 1) * dg)
                h = hn_ref[pl.ds(t * nseq + n0, ROWS), cols]
                s = h
                for j in range(1, w):
                    i = lp + t - j
                    if i < lp:
                        s = s + hp_refs[i][pl.ds(n0, ROWS), cols]
                    else:
                        s = s + hn_ref[pl.ds((i - lp) * nseq + n0, ROWS), cols]
                cnt = float(min(pos0 + t + 1, w))
                p_ref[pl.ds(t * nseq + n0, ROWS), cols] = (s / cnt - h).astype(BF16)
        return carry

    lax.fori_loop(0, nseq // ROWS, chunk, 0)


def _poolpre_sample(x, g, hist_p, nseq):
    m, d = x.shape
    blk = pl.BlockSpec((m, d), lambda i: (0, 0))
    ops_p, specs_p = _history_blocks(hist_p)
    return pl.pallas_call(
        functools.partial(_poolpre_sample_kernel, nseq=nseq, nt=m // nseq, pos0=PAST_LEN),
        grid=(1,),
        in_specs=[blk, _full(g)] + specs_p,
        out_specs=[blk, blk],
        out_shape=[jax.ShapeDtypeStruct((m, d), BF16), jax.ShapeDtypeStruct((m, d), F32)],
        compiler_params=_params("arbitrary"),
        name="poolpre_sample",
    )(x, g, *ops_p)


def _poolproj_kernel(x_ref, p_ref, w_ref, s_ref, o_ref):
    o_ref[...] = x_ref[...] + _bdot(p_ref[...], w_ref[0]) * s_ref[...]


def _poolproj(x, p, w, scale, tm):
    m, d = x.shape
    ng, dg = w.shape[0], w.shape[1]
    blk = pl.BlockSpec((tm, dg), lambda i, j: (i, j))
    return pl.pallas_call(
        _poolproj_kernel,
        grid=(m // tm, ng),
        in_specs=[blk, blk,
                  pl.BlockSpec((1, dg, dg), lambda i, j: (j, 0, 0)),
                  pl.BlockSpec((1, dg), lambda i, j: (0, j))],
        out_specs=blk,
        out_shape=jax.ShapeDtypeStruct((m, d), F32),
        compiler_params=_params("parallel", "arbitrary"),
        name="poolproj",
    )(x, p, w, scale)


def _mlp_kernel(x_ref, g_ref, wu_ref, wd_ref, gf_ref, o_ref, xn_ref, *, final):
    f = pl.program_id(1)

    @pl.when(f == 0)
    def _():
        x = x_ref[...]
        xn_ref[...] = _rms(x, g_ref[...]).astype(BF16)
        o_ref[...] = x

    h = jnp.square(jnp.maximum(_bdot(xn_ref[...], wu_ref[...]), 0.0))
    o_ref[...] += _bdot(h, wd_ref[...])

    if final:
        @pl.when(f == pl.num_programs(1) - 1)
        def _():
            o_ref[...] = _rms(o_ref[...], gf_ref[...])


def _mlp(x, g, wu, wd, gf, tm, tf=256, final=False):
    m, d = x.shape
    ff = wu.shape[1]
    vec = pl.BlockSpec((1, d), lambda i, f: (0, 0))
    return pl.pallas_call(
        functools.partial(_mlp_kernel, final=final),
        grid=(m // tm, ff // tf),
        in_specs=[pl.BlockSpec((tm, d), lambda i, f: (i, 0)), vec,
                  pl.BlockSpec((d, tf), lambda i, f: (0, f)),
                  pl.BlockSpec((tf, d), lambda i, f: (f, 0)), vec],
        out_specs=pl.BlockSpec((tm, d), lambda i, f: (i, 0)),
        out_shape=jax.ShapeDtypeStruct((m, d), F32),
        scratch_shapes=[pltpu.VMEM((tm, d), BF16)],
        compiler_params=_params("parallel", "arbitrary"),
        name="mlp",
    )(x, g, wu, wd, gf)


def _trunk(x, prompt, nseq, tm, tt, states, norm_mix, norm_mlp, norm_final, w_in_even, conv_a_w,
           conv_a_b, ln_a_g, ln_a_b, conv_b_w, w_out_even, pool_w, pool_scale, w_mlp_up, w_mlp_down):
    depth = norm_mix.shape[0]
    row = lambda vec: vec.reshape(1, -1)
    new_a, new_b, new_p = [], [], []
    for layer in range(depth):
        g = row(norm_mix[layer])
        if layer % 2 == 0:
            e = layer // 2
            a, v, bg = _inproj(x, g, w_in_even[e], tm)
            conv = (conv_a_w[e], row(conv_a_b[e]), row(ln_a_g[e]), row(ln_a_b[e]), conv_b_w[e])
            if prompt:
                u, a_tail, v_tail = _evenmix_prompt(a, v, bg, *conv, nseq=nseq, tt=tt)
                new_a.append(a_tail[:, -(conv_a_w.shape[1] - 1):])
                new_b.append(v_tail[:, -(conv_b_w.shape[1] - 1):])
            else:
                sa, sb = states[0][e], states[1][e]
                u = _evenmix_sample(a, v, bg, sa, sb, *conv, nseq=nseq)
                tmaj = lambda z: z.reshape(-1, nseq, z.shape[-1]).transpose(1, 0, 2)
                new_a.append(jnp.concatenate([sa, tmaj(a)], axis=1)[:, -sa.shape[1]:])
                new_b.append(jnp.concatenate([sb, tmaj(v)], axis=1)[:, -sb.shape[1]:])
            x = _outproj(x, u, w_out_even[e], tm)
        else:
            o = layer // 2
            if prompt:
                p, h_tail = _poolpre_prompt(x, g, nseq=nseq, tt=tt)
                new_p.append(h_tail[:, -(POOL_WINDOWS[-1] - 1):])
            else:
                sp = states[2][o]
                p, hn = _poolpre_sample(x, g, sp, nseq=nseq)
                hn = hn.reshape(-1, nseq, hn.shape[-1]).transpose(1, 0, 2)
                new_p.append(jnp.concatenate([sp, hn], axis=1)[:, -sp.shape[1]:])
            x = _poolproj(x, p, pool_w[o], row(pool_scale[o]), tm)
        x = _mlp(x, row(norm_mlp[layer]), w_mlp_up[layer], w_mlp_down[layer], row(norm_final), tm,
                 final=(layer == depth - 1))
    return x, jnp.stack(new_a), jnp.stack(new_b), jnp.stack(new_p)


def kernel(x_prompt, x_sample, state_conv_a, state_conv_b, state_pool, meta_tokens, norm_mix, norm_mlp, norm_final, w_in_even, conv_a_w, conv_a_b, ln_a_g, ln_a_b, conv_b_w, w_out_even, pool_w, pool_scale, w_mlp_up, w_mlp_down):
    params = (norm_mix, norm_mlp, norm_final, w_in_even, conv_a_w, conv_a_b, ln_a_g, ln_a_b,
              conv_b_w, w_out_even, pool_w, pool_scale, w_mlp_up, w_mlp_down)
    batch, seq, d = x_prompt.shape
    n_meta = meta_tokens.shape[0]
    t = n_meta + seq
    meta = jnp.broadcast_to(meta_tokens.astype(x_prompt.dtype)[None], (batch, n_meta, d))
    xp = jnp.concatenate([meta, x_prompt], axis=1).reshape(batch * t, d)
    tt = t // 3
    yp, na_p, nb_p, np_p = _trunk(xp, True, batch, tt, tt, None, *params)
    y_prompt = yp.reshape(batch, t, d)[:, n_meta:]

    nb, nt, _ = x_sample.shape
    xs = x_sample.transpose(1, 0, 2).reshape(nt * nb, d)
    ys, na_s, nb_s, np_s = _trunk(xs, False, nb, nt * nb, None,
                                  (state_conv_a, state_conv_b, state_pool), *params)
    y_sample = ys.reshape(nt, nb, d).transpose(1, 0, 2)
    return (y_prompt, y_sample, na_p, nb_p, np_p, na_s, nb_s, np_s)
```

```python
import functools

import jax
import jax.numpy as jnp
from jax import lax
from jax.experimental import pallas as pl
from jax.experimental.pallas import tpu as pltpu

F32 = jnp.float32
BF16 = jnp.bfloat16
EPS = 1e-6
PAST_LEN = 16384
POOL_WINDOWS = (2, 4, 8, 16)
VMEM_LIMIT = 58 * 1024 * 1024
ROWS = 16
SUBLANES = 8


def _params(*sem):
    return pltpu.CompilerParams(dimension_semantics=sem, vmem_limit_bytes=VMEM_LIMIT)


def _rms(x, g):
    ms = jnp.mean(x * x, axis=-1, keepdims=True)
    return x * lax.rsqrt(ms + EPS) * g


def _bdot(a, w):
    return jnp.dot(a.astype(BF16), w.astype(BF16), preferred_element_type=F32)


def _layer_spec(stacked, layer, nd_grid):
    tail = stacked.shape[1:]
    zeros = (0,) * len(tail)
    return pl.BlockSpec((None,) + tail, lambda *_: (layer,) + zeros)


def _inproj_kernel(x_ref, g_ref, w0, w1, w2, w3, w4, a_ref, v_ref, bg_ref, xn_ref):
    @pl.when(pl.program_id(1) == 0)
    def _():
        xn_ref[...] = _rms(x_ref[...], g_ref[...]).astype(BF16)

    xb = xn_ref[...]
    a_ref[...] = _bdot(xb, w0[...]) * jax.nn.sigmoid(_bdot(xb, w1[...]))
    bg_ref[...] = _bdot(xb, w2[...])
    v_ref[...] = _bdot(xb, w3[...]) * _bdot(xb, w4[...])


def _inproj(x, g, w_in, e, layer, tm, tn=512):
    m, d = x.shape
    dc = w_in.shape[2] // 5
    nj = dc // tn
    wspec = lambda grp: pl.BlockSpec((None, d, tn), lambda i, j, grp=grp: (e, 0, grp * nj + j))
    ospec = pl.BlockSpec((tm, tn), lambda i, j: (i, j))
    return pl.pallas_call(
        _inproj_kernel,
        grid=(m // tm, nj),
        in_specs=[pl.BlockSpec((tm, d), lambda i, j: (i, 0)), _layer_spec(g, layer, 2)]
                 + [wspec(k) for k in range(5)],
        out_specs=[ospec, ospec, ospec],
        out_shape=[jax.ShapeDtypeStruct((m, dc), F32)] * 3,
        scratch_shapes=[pltpu.VMEM((tm, d), BF16)],
        compiler_params=_params("parallel", "arbitrary"),
        name="inproj",
    )(x, g, w_in, w_in, w_in, w_in, w_in)


def _ln_silu(acc, g, b):
    mu = jnp.mean(acc, axis=-1, keepdims=True)
    xc = acc - mu
    var = jnp.mean(xc * xc, axis=-1, keepdims=True)
    y = xc * lax.rsqrt(var + EPS) * g + b
    return y * jax.nn.sigmoid(y)


def _evenmix_prompt_kernel(a_ref, v_ref, bg_ref, wa_ref, ba_ref, lg_ref, lb_ref, wb_ref,
                           u_ref, at_ref, vt_ref, aext, vext, ash, vsh, cvec, *, tt, ka, kb):
    tc = pl.program_id(1)
    ha = aext.shape[0] - tt
    hb = vext.shape[0] - tt
    dc = a_ref.shape[1]

    @pl.when(tc == 0)
    def _():
        aext[0:ha, :] = jnp.zeros((ha, dc), F32)
        vext[0:hb, :] = jnp.zeros((hb, dc), F32)

    @pl.when(tc > 0)
    def _():
        aext[0:ha, :] = aext[tt:tt + ha, :]
        vext[0:hb, :] = vext[tt:tt + hb, :]

    aext[ha:ha + tt, :] = a_ref[...]
    vext[hb:hb + tt, :] = v_ref[...]

    la = ash.shape[1]
    for r in range(1, SUBLANES):
        ash[r - 1] = aext[r:r + la, :]
    offs_b = [hb - kb + 1 + k for k in range(kb)]
    for s, o in enumerate(o for o in offs_b if o % SUBLANES):
        vsh[s] = vext[o:o + tt, :]

    @pl.when((pl.program_id(0) == 0) & (tc == 0))
    def _():
        for k in range(ka):
            cvec[k] = jnp.broadcast_to(wa_ref[k:k + 1, :], (SUBLANES, dc))
        for k in range(kb):
            cvec[ka + k] = jnp.broadcast_to(wb_ref[k:k + 1, :], (SUBLANES, dc))
        for k, ref in enumerate((ba_ref, lg_ref, lb_ref)):
            cvec[ka + kb + k] = jnp.broadcast_to(ref[...], (SUBLANES, dc))

    halves = range(0, ROWS, SUBLANES)

    def conv_a(i):
        base = pl.multiple_of(i * ROWS, ROWS)
        acc = [cvec[ka + kb] for _ in halves]
        for k in range(ka):
            q, r = divmod(ha - ka + 1 + k, SUBLANES)
            for n, h in enumerate(halves):
                rows = pl.ds(base + q * SUBLANES + h, SUBLANES)
                src = aext[rows, :] if r == 0 else ash[r - 1, rows, :]
                acc[n] = acc[n] + cvec[k] * src
        return tuple(acc)

    def finish(i, acc):
        base = pl.multiple_of(i * ROWS, ROWS)
        a_out = [_ln_silu(z, cvec[ka + kb + 1], cvec[ka + kb + 2]) for z in acc]
        u_ref[pl.ds(base, ROWS), 0:dc] = jnp.concatenate(a_out, axis=0).astype(BF16)
        b_out = []
        for h in halves:
            vc, s = None, 0
            for k, o in enumerate(offs_b):
                if o % SUBLANES:
                    src = vsh[s, pl.ds(base + h, SUBLANES), :]
                    s += 1
                else:
                    src = vext[pl.ds(base + o + h, SUBLANES), :]
                term = cvec[ka + k] * src
                vc = term if vc is None else vc + term
            b_out.append(bg_ref[pl.ds(base + h, SUBLANES), :] * vc)
        u_ref[pl.ds(base, ROWS), dc:2 * dc] = jnp.concatenate(b_out, axis=0).astype(BF16)

    def chunk(i, acc_prev):
        acc = conv_a(i)
        finish(i - 1, acc_prev)
        return acc

    nchunks = tt // ROWS
    finish(nchunks - 1, lax.fori_loop(1, nchunks, chunk, conv_a(0)))

    @pl.when(tc == pl.num_programs(1) - 1)
    def _():
        at_ref[0] = aext[tt:tt + ha, :]
        vt_ref[0] = vext[tt:tt + hb, :]


def _evenmix_prompt(a, v, bg, conv, e, nseq, tt):
    m, dc = a.shape
    t = m // nseq
    nt = t // tt
    ka, kb = conv[0].shape[1], conv[4].shape[1]
    ha, hb = 32, 8
    row = pl.BlockSpec((tt, dc), lambda b, c: (b * nt + c, 0))
    return pl.pallas_call(
        functools.partial(_evenmix_prompt_kernel, tt=tt, ka=ka, kb=kb),
        grid=(nseq, nt),
        in_specs=[row, row, row] + [_layer_spec(z, e, 2) for z in conv],
        out_specs=[pl.BlockSpec((tt, 2 * dc), lambda b, c: (b * nt + c, 0)),
                   pl.BlockSpec((1, ha, dc), lambda b, c: (b, 0, 0)),
                   pl.BlockSpec((1, hb, dc), lambda b, c: (b, 0, 0))],
        out_shape=[jax.ShapeDtypeStruct((m, 2 * dc), BF16),
                   jax.ShapeDtypeStruct((nseq, ha, dc), F32),
                   jax.ShapeDtypeStruct((nseq, hb, dc), F32)],
        scratch_shapes=[pltpu.VMEM((ha + tt, dc), F32), pltpu.VMEM((hb + tt, dc), F32),
                        pltpu.VMEM((SUBLANES - 1, tt + ha - SUBLANES, dc), F32),
                        pltpu.VMEM((sum(1 for k in range(kb) if (hb - kb + 1 + k) % SUBLANES), tt, dc), F32),
                        pltpu.VMEM((ka + kb + 3, SUBLANES, dc), F32)],
        compiler_params=_params("arbitrary", "arbitrary"),
        name="evenmix_prompt",
    )(a, v, bg, *conv)


def _evenmix_sample_kernel(*refs, nseq, nt, ka, kb):
    la, lb_ = ka - 1, kb - 1
    a_ref, v_ref, bg_ref = refs[0:3]
    ha_refs = refs[3:3 + la]
    hb_refs = refs[3 + la:3 + la + lb_]
    wa_ref, ba_ref, lg_ref, lb_ref, wb_ref, u_ref = refs[3 + la + lb_:]
    dc = a_ref.shape[1]

    def chunk(c, carry):
        n0 = pl.multiple_of(c * ROWS, ROWS)
        for t in range(nt):
            acc = jnp.broadcast_to(ba_ref[...], (ROWS, dc))
            for k in range(ka):
                j = t + k
                if j < la:
                    src = ha_refs[j][pl.ds(n0, ROWS), :]
                else:
                    src = a_ref[pl.ds((j - la) * nseq + n0, ROWS), :]
                acc = acc + wa_ref[k:k + 1, :] * src
            rows = pl.ds(t * nseq + n0, ROWS)
            u_ref[rows, 0:dc] = _ln_silu(acc, lg_ref[...], lb_ref[...]).astype(BF16)
            vc = None
            for k in range(kb):
                j = t + k
                if j < lb_:
                    src = hb_refs[j][pl.ds(n0, ROWS), :]
                else:
                    src = v_ref[pl.ds((j - lb_) * nseq + n0, ROWS), :]
                term = wb_ref[k:k + 1, :] * src
                vc = term if vc is None else vc + term
            u_ref[rows, dc:2 * dc] = (bg_ref[rows, :] * vc).astype(BF16)
        return carry

    lax.fori_loop(0, nseq // ROWS, chunk, 0)


def _history_blocks(state, idx):
    nl, nseq, rows, ch = state.shape
    flat = state.reshape(nl, nseq, rows * ch)
    return ([flat] * rows,
            [pl.BlockSpec((None, nseq, ch), lambda i, j=j: (idx, 0, j)) for j in range(rows)])


def _full(arr):
    return pl.BlockSpec(arr.shape, lambda i: (0,) * arr.ndim)


def _evenmix_sample(a, v, bg, state_a, state_b, conv, e, nseq):
    m, dc = a.shape
    nt = m // nseq
    ka, kb = conv[0].shape[1], conv[4].shape[1]
    ops_a, specs_a = _history_blocks(state_a, e)
    ops_b, specs_b = _history_blocks(state_b, e)
    return pl.pallas_call(
        functools.partial(_evenmix_sample_kernel, nseq=nseq, nt=nt, ka=ka, kb=kb),
        grid=(1,),
        in_specs=[_full(a), _full(v), _full(bg)] + specs_a + specs_b
                 + [_layer_spec(z, e, 1) for z in conv],
        out_specs=pl.BlockSpec((m, 2 * dc), lambda i: (0, 0)),
        out_shape=jax.ShapeDtypeStruct((m, 2 * dc), BF16),
        compiler_params=_params("arbitrary"),
        name="evenmix_sample",
    )(a, v, bg, *ops_a, *ops_b, *conv)


def _outproj_kernel(x_ref, u_ref, w_ref, o_ref):
    o_ref[...] = x_ref[...] + _bdot(u_ref[...], w_ref[...])


def _outproj(x, u, w, e, tm, tn=512):
    m, d = x.shape
    return pl.pallas_call(
        _outproj_kernel,
        grid=(m // tm, d // tn),
        in_specs=[pl.BlockSpec((tm, tn), lambda i, j: (i, j)),
                  pl.BlockSpec((tm, u.shape[1]), lambda i, j: (i, 0)),
                  pl.BlockSpec((None, w.shape[1], tn), lambda i, j: (e, 0, j))],
        out_specs=pl.BlockSpec((tm, tn), lambda i, j: (i, j)),
        out_shape=jax.ShapeDtypeStruct((m, d), F32),
        compiler_params=_params("parallel", "arbitrary"),
        name="outproj",
    )(x, u, w)


def _pool_project(x_ref, p_ref, w_ref, s_ref, o_ref):
    dg = w_ref.shape[1]
    for gi in range(w_ref.shape[0]):
        cols = slice(gi * dg, (gi + 1) * dg)
        o_ref[:, cols] = x_ref[:, cols] + _bdot(p_ref[:, cols], w_ref[gi]) * s_ref[:, cols]


def _poolmix_prompt_kernel(x_ref, g_ref, w_ref, sc_ref, o_ref, ht_ref, hext, s2, s4, s8, p_ref, *, tt):
    assert POOL_WINDOWS == (2, 4, 8, 16)
    tc = pl.program_id(1)
    hh = hext.shape[0] - tt
    end = hh + tt
    d = x_ref.shape[1]
    dg = d // len(POOL_WINDOWS)
    lo = SUBLANES

    @pl.when(tc == 0)
    def _():
        hext[0:hh, :] = jnp.zeros((hh, d), F32)

    @pl.when(tc > 0)
    def _():
        hext[0:hh, :] = hext[tt:tt + hh, :]

    hext[hh:end, :] = _rms(x_ref[...], g_ref[...])
    for stage in (s2, s4, s8):
        stage[0:lo, :] = jnp.zeros((lo, stage.shape[1]), F32)
    s2[lo:end, :] = hext[lo:end, dg:] + hext[lo - 1:end - 1, dg:]
    s4[lo:end, :] = s2[lo:end, dg:] + s2[lo - 2:end - 2, dg:]
    s8[lo:end, :] = s4[lo:end, dg:] + s4[lo - 4:end - 4, dg:]
    sums = (hext[hh:end, 0:dg] + hext[hh - 1:end - 1, 0:dg],
            s2[hh:end, 0:dg] + s2[hh - 2:end - 2, 0:dg],
            s4[hh:end, 0:dg] + s4[hh - 4:end - 4, 0:dg],
            s8[hh:end, 0:dg] + s8[hh - 8:end - 8, 0:dg])
    pos = tc * tt + lax.broadcasted_iota(jnp.int32, (tt, 1), 0)
    for gi, w in enumerate(POOL_WINDOWS):
        cols = slice(gi * dg, (gi + 1) * dg)
        cnt = jnp.minimum(pos + 1, w).astype(F32)
        p_ref[:, cols] = (sums[gi] / cnt - hext[hh:end, cols]).astype(BF16)

    _pool_project(x_ref, p_ref, w_ref, sc_ref, o_ref)

    @pl.when(tc == pl.num_programs(1) - 1)
    def _():
        ht_ref[0] = hext[tt:end, :]


def _poolmix_prompt(x, g, pw, scale, o, layer, nseq, tt):
    m, d = x.shape
    nt = (m // nseq) // tt
    hh = 24
    dg = d // len(POOL_WINDOWS)
    tile = pl.BlockSpec((tt, d), lambda b, c: (b * nt + c, 0))
    return pl.pallas_call(
        functools.partial(_poolmix_prompt_kernel, tt=tt),
        grid=(nseq, nt),
        in_specs=[tile, _layer_spec(g, layer, 2), _layer_spec(pw, o, 2), _layer_spec(scale, o, 2)],
        out_specs=[tile, pl.BlockSpec((1, hh, d), lambda b, c: (b, 0, 0))],
        out_shape=[jax.ShapeDtypeStruct((m, d), F32),
                   jax.ShapeDtypeStruct((nseq, hh, d), F32)],
        scratch_shapes=[pltpu.VMEM((hh + tt, d), F32), pltpu.VMEM((hh + tt, 3 * dg), F32),
                        pltpu.VMEM((hh + tt, 2 * dg), F32), pltpu.VMEM((hh + tt, dg), F32),
                        pltpu.VMEM((tt, d), BF16)],
        compiler_params=_params("arbitrary", "arbitrary"),
        name="poolmix_prompt",
    )(x, g, pw, scale)


def _poolmix_sample_kernel(*refs, nseq, nt, pos0):
    lp = POOL_WINDOWS[-1] - 1
    x_ref, g_ref, w_ref, sc_ref = refs[0:4]
    hp_refs = refs[4:4 + lp]
    o_ref, hn_ref, p_ref = refs[4 + lp:]
    d = x_ref.shape[1]
    dg = d // len(POOL_WINDOWS)
    hn_ref[...] = _rms(x_ref[...], g_ref[...])

    def chunk(c, carry):
        n0 = pl.multiple_of(c * ROWS, ROWS)
        for t in range(nt):
            for gi, w in enumerate(POOL_WINDOWS):
                cols = slice(gi * dg, (gi + 1) * dg)
                h = hn_ref[pl.ds(t * nseq + n0, ROWS), cols]
                s = h
                for j in range(1, w):
                    i = lp + t - j
                    if i < lp:
                        s = s + hp_refs[i][pl.ds(n0, ROWS), cols]
                    else:
                        s = s + hn_ref[pl.ds((i - lp) * nseq + n0, ROWS), cols]
                cnt = float(min(pos0 + t + 1, w))
                p_ref[pl.ds(t * nseq + n0, ROWS), cols] = (s / cnt - h).astype(BF16)
        return carry

    lax.fori_loop(0, nseq // ROWS, chunk, 0)
    _pool_project(x_ref, p_ref, w_ref, sc_ref, o_ref)


def _poolmix_sample(x, g, pw, scale, state_p, o, layer, nseq):
    m, d = x.shape
    blk = pl.BlockSpec((m, d), lambda i: (0, 0))
    ops_p, specs_p = _history_blocks(state_p, o)
    return pl.pallas_call(
        functools.partial(_poolmix_sample_kernel, nseq=nseq, nt=m // nseq, pos0=PAST_LEN),
        grid=(1,),
        in_specs=[blk, _layer_spec(g, layer, 1), _layer_spec(pw, o, 1), _layer_spec(scale, o, 1)] + specs_p,
        out_specs=[blk, blk],
        out_shape=[jax.ShapeDtypeStruct((m, d), F32), jax.ShapeDtypeStruct((m, d), F32)],
        scratch_shapes=[pltpu.VMEM((m, d), BF16)],
        compiler_params=_params("arbitrary"),
        name="poolmix_sample",
    )(x, g, pw, scale, *ops_p)


def _mlp_kernel(x_ref, xr_ref, g_ref, wu_ref, wd_ref, o_ref, xn_ref, h_ref, *, nf):
    s = pl.program_id(1)
    tf = h_ref.shape[2]

    @pl.when(s == 0)
    def _():
        xn_ref[...] = _rms(x_ref[...], g_ref[...]).astype(BF16)

    @pl.when(s < nf)
    def _():
        up = jnp.dot(xn_ref[...], wu_ref[...], preferred_element_type=F32)
        h_ref[s] = jnp.square(jnp.maximum(up, 0.0)).astype(BF16)

    @pl.when(s >= nf)
    def _():
        acc = xr_ref[...]
        for c in range(nf):
            acc = acc + jnp.dot(h_ref[c], wd_ref[c * tf:(c + 1) * tf, :], preferred_element_type=F32)
        o_ref[...] = acc


def _mlp(x, g, wu, wd, layer, tm, tf=1024, tn=256):
    m, d = x.shape
    ff = wu.shape[2]
    nf, nn = ff // tf, d // tn
    col = lambda i, s: (i, jnp.maximum(s - nf, 0))
    return pl.pallas_call(
        functools.partial(_mlp_kernel, nf=nf),
        grid=(m // tm, nf + nn),
        in_specs=[pl.BlockSpec((tm, d), lambda i, s: (i, 0)),
                  pl.BlockSpec((tm, tn), col),
                  _layer_spec(g, layer, 2),
                  pl.BlockSpec((None, d, tf), lambda i, s: (layer, 0, jnp.minimum(s, nf - 1))),
                  pl.BlockSpec((None, ff, tn), lambda i, s: (layer, 0, jnp.maximum(s - nf, 0)))],
        out_specs=pl.BlockSpec((tm, tn), col),
        out_shape=jax.ShapeDtypeStruct((m, d), F32),
        scratch_shapes=[pltpu.VMEM((tm, d), BF16), pltpu.VMEM((nf, tm, tf), BF16)],
        compiler_params=_params("parallel", "arbitrary"),
        name="mlp",
    )(x, x, g, wu, wd)


def _final_kernel(x_ref, g_ref, o_ref):
    o_ref[...] = _rms(x_ref[...], g_ref[...])


def _final_prompt(x, g, nseq, n_meta, ts=512):
    m, d = x.shape
    t = m // nseq
    seq = t - n_meta
    return pl.pallas_call(
        _final_kernel,
        grid=(nseq, seq // ts),
        in_specs=[pl.BlockSpec((pl.Element(1), pl.Element(ts), pl.Element(d)),
                               lambda b, j: (b, pl.multiple_of(n_meta + j * ts, SUBLANES), 0)),
                  pl.BlockSpec((1, d), lambda b, j: (0, 0))],
        out_specs=pl.BlockSpec((1, ts, d), lambda b, j: (b, j, 0)),
        out_shape=jax.ShapeDtypeStruct((nseq, seq, d), F32),
        compiler_params=_params("parallel", "parallel"),
        name="final_prompt",
    )(x.reshape(nseq, t, d), g)


def _final_sample(x, g, nseq):
    m, d = x.shape
    nt = m // nseq
    y = pl.pallas_call(
        _final_kernel,
        grid=(nt,),
        in_specs=[pl.BlockSpec((nseq, d), lambda t: (t, 0)), pl.BlockSpec((1, d), lambda t: (0, 0))],
        out_specs=pl.BlockSpec((nseq, d), lambda t: (0, t)),
        out_shape=jax.ShapeDtypeStruct((nseq, nt * d), F32),
        compiler_params=_params("parallel"),
        name="final_sample",
    )(x, g)
    return y.reshape(nseq, nt, d)


def _trunk(x, prompt, nseq, tm, tt, states, p):
    depth = p["norm_mix"].shape[0]
    conv = (p["conv_a_w"], p["conv_a_b"], p["ln_a_g"], p["ln_a_b"], p["conv_b_w"])
    new_a, new_b, new_p = [], [], []
    tmaj = lambda z: z.reshape(-1, nseq, z.shape[-1]).transpose(1, 0, 2)
    for layer in range(depth):
        if layer % 2 == 0:
            e = layer // 2
            a, v, bg = _inproj(x, p["norm_mix"], p["w_in_even"], e, layer, tm)
            if prompt:
                u, a_tail, v_tail = _evenmix_prompt(a, v, bg, conv, e, nseq=nseq, tt=tt)
                new_a.append(a_tail[:, -(conv[0].shape[1] - 1):])
                new_b.append(v_tail[:, -(conv[4].shape[1] - 1):])
            else:
                u = _evenmix_sample(a, v, bg, states[0], states[1], conv, e, nseq=nseq)
                sa, sb = states[0][e], states[1][e]
                new_a.append(jnp.concatenate([sa, tmaj(a)], axis=1)[:, -sa.shape[1]:])
                new_b.append(jnp.concatenate([sb, tmaj(v)], axis=1)[:, -sb.shape[1]:])
            x = _outproj(x, u, p["w_out_even"], e, tm)
        else:
            o = layer // 2
            if prompt:
                x, h_tail = _poolmix_prompt(x, p["norm_mix"], p["pool_w"], p["pool_scale"], o, layer,
                                            nseq=nseq, tt=tt)
                new_p.append(h_tail[:, -(POOL_WINDOWS[-1] - 1):])
            else:
                x, hn = _poolmix_sample(x, p["norm_mix"], p["pool_w"], p["pool_scale"], states[2], o,
                                        layer, nseq=nseq)
                sp = states[2][o]
                new_p.append(jnp.concatenate([sp, tmaj(hn)], axis=1)[:, -sp.shape[1]:])
        x = _mlp(x, p["norm_mlp"], p["w_mlp_up"], p["w_mlp_down"], layer, tm)
    return x, jnp.stack(new_a), jnp.stack(new_b), jnp.stack(new_p)


def kernel(x_prompt, x_sample, state_conv_a, state_conv_b, state_pool, meta_tokens, norm_mix, norm_mlp, norm_final, w_in_even, conv_a_w, conv_a_b, ln_a_g, ln_a_b, conv_b_w, w_out_even, pool_w, pool_scale, w_mlp_up, w_mlp_down):
    vec = lambda z: z.reshape(z.shape[0], 1, z.shape[1])
    p = dict(norm_mix=vec(norm_mix), norm_mlp=vec(norm_mlp),
             w_in_even=w_in_even.astype(BF16), conv_a_w=conv_a_w, conv_a_b=vec(conv_a_b),
             ln_a_g=vec(ln_a_g), ln_a_b=vec(ln_a_b), conv_b_w=conv_b_w,
             w_out_even=w_out_even.astype(BF16), pool_w=pool_w.astype(BF16), pool_scale=vec(pool_scale),
             w_mlp_up=w_mlp_up.astype(BF16), w_mlp_down=w_mlp_down.astype(BF16))
    gf = norm_final.reshape(1, -1)
    batch, seq, d = x_prompt.shape
    n_meta = meta_tokens.shape[0]
    t = n_meta + seq
    meta = jnp.broadcast_to(meta_tokens.astype(x_prompt.dtype)[None], (batch, n_meta, d))
    xp = jnp.concatenate([meta, x_prompt], axis=1).reshape(batch * t, d)
    tt = t // 3
    yp, na_p, nb_p, np_p = _trunk(xp, True, batch, tt, tt, None, p)
    y_prompt = _final_prompt(yp, gf, batch, n_meta)

    nb, nt, _ = x_sample.shape
    xs = x_sample.transpose(1, 0, 2).reshape(nt * nb, d)
    ys, na_s, nb_s, np_s = _trunk(xs, False, nb, nt * nb, None,
                                  (state_conv_a, state_conv_b, state_pool), p)
    y_sample = _final_sample(ys, gf, nb)
    return (y_prompt, y_sample, na_p, nb_p, np_p, na_s, nb_s, np_s)
```

```python
import functools

import jax
import jax.numpy as jnp
from jax import lax
from jax.experimental import pallas as pl
from jax.experimental.pallas import tpu as pltpu

F32 = jnp.float32
BF16 = jnp.bfloat16
EPS = 1e-6
PAST_LEN = 16384
POOL_WINDOWS = (2, 4, 8, 16)
VMEM_LIMIT = 58 * 1024 * 1024
ROWS = 16
SUBLANES = 8


def _params(*sem):
    return pltpu.CompilerParams(dimension_semantics=sem, vmem_limit_bytes=VMEM_LIMIT)


def _rms(x, g):
    ms = jnp.mean(x * x, axis=-1, keepdims=True)
    return x * lax.rsqrt(ms + EPS) * g


def _bdot(a, w):
    return jnp.dot(a.astype(BF16), w.astype(BF16), preferred_element_type=F32)


def _layer_spec(stacked, layer, nd_grid):
    tail = stacked.shape[1:]
    zeros = (0,) * len(tail)
    return pl.BlockSpec((None,) + tail, lambda *_: (layer,) + zeros)


N_IN_GROUPS = 5


def _inproj_kernel(*refs, emit_bf16):
    x_ref, g_ref = refs[0:2]
    w_refs = refs[2:2 + N_IN_GROUPS]
    a_ref, v_ref, bg_ref = refs[2 + N_IN_GROUPS:5 + N_IN_GROUPS]
    xn_ref = refs[-1]

    @pl.when(pl.program_id(1) == 0)
    def _():
        xn_ref[...] = _rms(x_ref[...], g_ref[...]).astype(BF16)

    w = [r[...].astype(BF16) for r in w_refs]
    if emit_bf16:
        for wb_ref, wb in zip(refs[5 + N_IN_GROUPS:-1], w):
            wb_ref[...] = wb
    xb = xn_ref[...]
    dot = lambda k: jnp.dot(xb, w[k], preferred_element_type=F32)
    a_ref[...] = dot(0) * jax.nn.sigmoid(dot(1))
    bg_ref[...] = dot(2)
    v_ref[...] = dot(3) * dot(4)


def _inproj(x, g, w_groups, layer, tm, tn=512):
    m, d = x.shape
    dc = w_groups[0].shape[1]
    ospec = pl.BlockSpec((tm, tn), lambda i, j: (i, j))
    return pl.pallas_call(
        functools.partial(_inproj_kernel, emit_bf16=False),
        grid=(m // tm, dc // tn),
        in_specs=[pl.BlockSpec((tm, d), lambda i, j: (i, 0)), _layer_spec(g, layer, 2)]
                 + [pl.BlockSpec((d, tn), lambda i, j: (0, j))] * N_IN_GROUPS,
        out_specs=[ospec, ospec, ospec],
        out_shape=[jax.ShapeDtypeStruct((m, dc), F32)] * 3,
        scratch_shapes=[pltpu.VMEM((tm, d), BF16)],
        compiler_params=_params("parallel", "arbitrary"),
        name="inproj",
    )(x, g, *w_groups)


def _inproj_cast(x, g, w_in, e, layer, tn=256):
    m, d = x.shape
    dc = w_in.shape[2] // N_IN_GROUPS
    nj = dc // tn
    wspec = lambda grp: pl.BlockSpec((None, d, tn), lambda i, j, grp=grp: (e, 0, grp * nj + j))
    ospec = pl.BlockSpec((m, tn), lambda i, j: (i, j))
    outs = pl.pallas_call(
        functools.partial(_inproj_kernel, emit_bf16=True),
        grid=(1, nj),
        in_specs=[pl.BlockSpec((m, d), lambda i, j: (i, 0)), _layer_spec(g, layer, 2)]
                 + [wspec(k) for k in range(N_IN_GROUPS)],
        out_specs=[ospec, ospec, ospec] + [pl.BlockSpec((d, tn), lambda i, j: (0, j))] * N_IN_GROUPS,
        out_shape=[jax.ShapeDtypeStruct((m, dc), F32)] * 3
                  + [jax.ShapeDtypeStruct((d, dc), BF16)] * N_IN_GROUPS,
        scratch_shapes=[pltpu.VMEM((m, d), BF16)],
        compiler_params=_params("arbitrary", "arbitrary"),
        name="inproj_cast",
    )(x, g, *([w_in] * N_IN_GROUPS))
    return outs[0:3], outs[3:]


def _ln_silu(acc, g, b):
    mu = jnp.mean(acc, axis=-1, keepdims=True)
    xc = acc - mu
    var = jnp.mean(xc * xc, axis=-1, keepdims=True)
    y = xc * lax.rsqrt(var + EPS) * g + b
    return y * jax.nn.sigmoid(y)


def _evenmix_prompt_kernel(a_ref, v_ref, bg_ref, wa_ref, ba_ref, lg_ref, lb_ref, wb_ref,
                           u_ref, at_ref, vt_ref, aext, vext, ash, vsh, cvec, *, tt, ka, kb):
    tc = pl.program_id(1)
    ha = aext.shape[0] - tt
    hb = vext.shape[0] - tt
    dc = a_ref.shape[1]

    @pl.when(tc == 0)
    def _():
        aext[0:ha, :] = jnp.zeros((ha, dc), F32)
        vext[0:hb, :] = jnp.zeros((hb, dc), F32)

    @pl.when(tc > 0)
    def _():
        aext[0:ha, :] = aext[tt:tt + ha, :]
        vext[0:hb, :] = vext[tt:tt + hb, :]

    aext[ha:ha + tt, :] = a_ref[...]
    vext[hb:hb + tt, :] = v_ref[...]

    la = ash.shape[1]
    for r in range(1, SUBLANES):
        ash[r - 1] = aext[r:r + la, :]
    offs_b = [hb - kb + 1 + k for k in range(kb)]
    for s, o in enumerate(o for o in offs_b if o % SUBLANES):
        vsh[s] = vext[o:o + tt, :]

    @pl.when((pl.program_id(0) == 0) & (tc == 0))
    def _():
        for k in range(ka):
            cvec[k] = jnp.broadcast_to(wa_ref[k:k + 1, :], (SUBLANES, dc))
        for k in range(kb):
            cvec[ka + k] = jnp.broadcast_to(wb_ref[k:k + 1, :], (SUBLANES, dc))
        for k, ref in enumerate((ba_ref, lg_ref, lb_ref)):
            cvec[ka + kb + k] = jnp.broadcast_to(ref[...], (SUBLANES, dc))

    halves = range(0, ROWS, SUBLANES)

    def conv_a(i):
        base = pl.multiple_of(i * ROWS, ROWS)
        acc = [cvec[ka + kb] for _ in halves]
        for k in range(ka):
            q, r = divmod(ha - ka + 1 + k, SUBLANES)
            for n, h in enumerate(halves):
                rows = pl.ds(base + q * SUBLANES + h, SUBLANES)
                src = aext[rows, :] if r == 0 else ash[r - 1, rows, :]
                acc[n] = acc[n] + cvec[k] * src
        return tuple(acc)

    def finish(i, acc):
        base = pl.multiple_of(i * ROWS, ROWS)
        a_out = [_ln_silu(z, cvec[ka + kb + 1], cvec[ka + kb + 2]) for z in acc]
        u_ref[pl.ds(base, ROWS), 0:dc] = jnp.concatenate(a_out, axis=0).astype(BF16)
        b_out = []
        for h in halves:
            vc, s = None, 0
            for k, o in enumerate(offs_b):
                if o % SUBLANES:
                    src = vsh[s, pl.ds(base + h, SUBLANES), :]
                    s += 1
                else:
                    src = vext[pl.ds(base + o + h, SUBLANES), :]
                term = cvec[ka + k] * src
                vc = term if vc is None else vc + term
            b_out.append(bg_ref[pl.ds(base + h, SUBLANES), :] * vc)
        u_ref[pl.ds(base, ROWS), dc:2 * dc] = jnp.concatenate(b_out, axis=0).astype(BF16)

    def chunk(i, acc_prev):
        acc = conv_a(i)
        finish(i - 1, acc_prev)
        return acc

    nchunks = tt // ROWS
    finish(nchunks - 1, lax.fori_loop(1, nchunks, chunk, conv_a(0)))

    @pl.when(tc == pl.num_programs(1) - 1)
    def _():
        at_ref[0] = aext[tt:tt + ha, :]
        vt_ref[0] = vext[tt:tt + hb, :]


def _evenmix_prompt(a, v, bg, conv, e, nseq, tt):
    m, dc = a.shape
    t = m // nseq
    nt = t // tt
    ka, kb = conv[0].shape[1], conv[4].shape[1]
    ha, hb = 32, 8
    row = pl.BlockSpec((tt, dc), lambda b, c: (b * nt + c, 0))
    return pl.pallas_call(
        functools.partial(_evenmix_prompt_kernel, tt=tt, ka=ka, kb=kb),
        grid=(nseq, nt),
        in_specs=[row, row, row] + [_layer_spec(z, e, 2) for z in conv],
        out_specs=[pl.BlockSpec((tt, 2 * dc), lambda b, c: (b * nt + c, 0)),
                   pl.BlockSpec((1, ha, dc), lambda b, c: (b, 0, 0)),
                   pl.BlockSpec((1, hb, dc), lambda b, c: (b, 0, 0))],
        out_shape=[jax.ShapeDtypeStruct((m, 2 * dc), BF16),
                   jax.ShapeDtypeStruct((nseq, ha, dc), F32),
                   jax.ShapeDtypeStruct((nseq, hb, dc), F32)],
        scratch_shapes=[pltpu.VMEM((ha + tt, dc), F32), pltpu.VMEM((hb + tt, dc), F32),
                        pltpu.VMEM((SUBLANES - 1, tt + ha - SUBLANES, dc), F32),
                        pltpu.VMEM((sum(1 for k in range(kb) if (hb - kb + 1 + k) % SUBLANES), tt, dc), F32),
                        pltpu.VMEM((ka + kb + 3, SUBLANES, dc), F32)],
        compiler_params=_params("arbitrary", "arbitrary"),
        name="evenmix_prompt",
    )(a, v, bg, *conv)


def _evenmix_sample_kernel(*refs, nseq, nt, ka, kb):
    la, lb_ = ka - 1, kb - 1
    a_ref, v_ref, bg_ref = refs[0:3]
    ha_refs = refs[3:3 + la]
    hb_refs = refs[3 + la:3 + la + lb_]
    wa_ref, ba_ref, lg_ref, lb_ref, wb_ref, u_ref = refs[3 + la + lb_:]
    dc = a_ref.shape[1]

    def chunk(c, carry):
        n0 = pl.multiple_of(c * ROWS, ROWS)
        for t in range(nt):
            acc = jnp.broadcast_to(ba_ref[...], (ROWS, dc))
            for k in range(ka):
                j = t + k
                if j < la:
                    src = ha_refs[j][pl.ds(n0, ROWS), :]
                else:
                    src = a_ref[pl.ds((j - la) * nseq + n0, ROWS), :]
                acc = acc + wa_ref[k:k + 1, :] * src
            rows = pl.ds(t * nseq + n0, ROWS)
            u_ref[rows, 0:dc] = _ln_silu(acc, lg_ref[...], lb_ref[...]).astype(BF16)
            vc = None
            for k in range(kb):
                j = t + k
                if j < lb_:
                    src = hb_refs[j][pl.ds(n0, ROWS), :]
                else:
                    src = v_ref[pl.ds((j - lb_) * nseq + n0, ROWS), :]
                term = wb_ref[k:k + 1, :] * src
                vc = term if vc is None else vc + term
            u_ref[rows, dc:2 * dc] = (bg_ref[rows, :] * vc).astype(BF16)
        return carry

    lax.fori_loop(0, nseq // ROWS, chunk, 0)


def _history_blocks(state, idx):
    nl, nseq, rows, ch = state.shape
    flat = state.reshape(nl, nseq, rows * ch)
    return ([flat] * rows,
            [pl.BlockSpec((None, nseq, ch), lambda i, j=j: (idx, 0, j)) for j in range(rows)])


def _full(arr):
    return pl.BlockSpec(arr.shape, lambda i: (0,) * arr.ndim)


def _evenmix_sample(a, v, bg, state_a, state_b, conv, e, nseq):
    m, dc = a.shape
    nt = m // nseq
    ka, kb = conv[0].shape[1], conv[4].shape[1]
    ops_a, specs_a = _history_blocks(state_a, e)
    ops_b, specs_b = _history_blocks(state_b, e)
    return pl.pallas_call(
        functools.partial(_evenmix_sample_kernel, nseq=nseq, nt=nt, ka=ka, kb=kb),
        grid=(1,),
        in_specs=[_full(a), _full(v), _full(bg)] + specs_a + specs_b
                 + [_layer_spec(z, e, 1) for z in conv],
        out_specs=pl.BlockSpec((m, 2 * dc), lambda i: (0, 0)),
        out_shape=jax.ShapeDtypeStruct((m, 2 * dc), BF16),
        compiler_params=_params("arbitrary"),
        name="evenmix_sample",
    )(a, v, bg, *ops_a, *ops_b, *conv)


def _outproj_cast_kernel(x_ref, u_ref, w_ref, o_ref, wb_ref):
    wb = w_ref[...].astype(BF16)
    wb_ref[...] = wb
    o_ref[...] = x_ref[...] + jnp.dot(u_ref[...], wb, preferred_element_type=F32)


def _outproj_cast(x, u, w, e, tn=512):
    m, d = x.shape
    k = w.shape[1]
    return pl.pallas_call(
        _outproj_cast_kernel,
        grid=(d // tn,),
        in_specs=[pl.BlockSpec((m, tn), lambda j: (0, j)),
                  pl.BlockSpec((m, k), lambda j: (0, 0)),
                  pl.BlockSpec((None, k, tn), lambda j: (e, 0, j))],
        out_specs=[pl.BlockSpec((m, tn), lambda j: (0, j)), pl.BlockSpec((k, tn), lambda j: (0, j))],
        out_shape=[jax.ShapeDtypeStruct((m, d), F32), jax.ShapeDtypeStruct((k, d), BF16)],
        compiler_params=_params("arbitrary"),
        name="outproj_cast",
    )(x, u, w)


def _pool_project(x_ref, p_ref, w_ref, s_ref, o_ref):
    dg = w_ref.shape[1]
    for gi in range(w_ref.shape[0]):
        cols = slice(gi * dg, (gi + 1) * dg)
        o_ref[:, cols] = x_ref[:, cols] + _bdot(p_ref[:, cols], w_ref[gi]) * s_ref[:, cols]


def _poolmix_prompt_kernel(x_ref, g_ref, w_ref, sc_ref, o_ref, ht_ref, hext, s2, s4, s8, p_ref, *, tt):
    assert POOL_WINDOWS == (2, 4, 8, 16)
    tc = pl.program_id(1)
    hh = hext.shape[0] - tt
    end = hh + tt
    d = x_ref.shape[1]
    dg = d // len(POOL_WINDOWS)
    lo = SUBLANES

    @pl.when(tc == 0)
    def _():
        hext[0:hh, :] = jnp.zeros((hh, d), F32)

    @pl.when(tc > 0)
    def _():
        hext[0:hh, :] = hext[tt:tt + hh, :]

    hext[hh:end, :] = _rms(x_ref[...], g_ref[...])
    for stage in (s2, s4, s8):
        stage[0:lo, :] = jnp.zeros((lo, stage.shape[1]), F32)
    s2[lo:end, :] = hext[lo:end, dg:] + hext[lo - 1:end - 1, dg:]
    s4[lo:end, :] = s2[lo:end, dg:] + s2[lo - 2:end - 2, dg:]
    s8[lo:end, :] = s4[lo:end, dg:] + s4[lo - 4:end - 4, dg:]
    sums = (hext[hh:end, 0:dg] + hext[hh - 1:end - 1, 0:dg],
            s2[hh:end, 0:dg] + s2[hh - 2:end - 2, 0:dg],
            s4[hh:end, 0:dg] + s4[hh - 4:end - 4, 0:dg],
            s8[hh:end, 0:dg] + s8[hh - 8:end - 8, 0:dg])
    pos = tc * tt + lax.broadcasted_iota(jnp.int32, (tt, 1), 0)
    for gi, w in enumerate(POOL_WINDOWS):
        cols = slice(gi * dg, (gi + 1) * dg)
        cnt = jnp.minimum(pos + 1, w).astype(F32)
        p_ref[:, cols] = (sums[gi] / cnt - hext[hh:end, cols]).astype(BF16)

    _pool_project(x_ref, p_ref, w_ref, sc_ref, o_ref)

    @pl.when(tc == pl.num_programs(1) - 1)
    def _():
        ht_ref[0] = hext[tt:end, :]


def _poolmix_prompt(x, g, pw, scale, o, layer, nseq, tt):
    m, d = x.shape
    nt = (m // nseq) // tt
    hh = 24
    dg = d // len(POOL_WINDOWS)
    tile = pl.BlockSpec((tt, d), lambda b, c: (b * nt + c, 0))
    return pl.pallas_call(
        functools.partial(_poolmix_prompt_kernel, tt=tt),
        grid=(nseq, nt),
        in_specs=[tile, _layer_spec(g, layer, 2), _layer_spec(pw, o, 2), _layer_spec(scale, o, 2)],
        out_specs=[tile, pl.BlockSpec((1, hh, d), lambda b, c: (b, 0, 0))],
        out_shape=[jax.ShapeDtypeStruct((m, d), F32),
                   jax.ShapeDtypeStruct((nseq, hh, d), F32)],
        scratch_shapes=[pltpu.VMEM((hh + tt, d), F32), pltpu.VMEM((hh + tt, 3 * dg), F32),
                        pltpu.VMEM((hh + tt, 2 * dg), F32), pltpu.VMEM((hh + tt, dg), F32),
                        pltpu.VMEM((tt, d), BF16)],
        compiler_params=_params("arbitrary", "arbitrary"),
        name="poolmix_prompt",
    )(x, g, pw, scale)


def _poolmix_sample_kernel(*refs, nseq, nt, pos0):
    lp = POOL_WINDOWS[-1] - 1
    x_ref, g_ref, w_ref, sc_ref = refs[0:4]
    hp_refs = refs[4:4 + lp]
    o_ref, hn_ref, p_ref = refs[4 + lp:]
    d = x_ref.shape[1]
    dg = d // len(POOL_WINDOWS)
    hn_ref[...] = _rms(x_ref[...], g_ref[...])

    def chunk(c, carry):
        n0 = pl.multiple_of(c * ROWS, ROWS)
        for t in range(nt):
            for gi, w in enumerate(POOL_WINDOWS):
                cols = slice(gi * dg, (gi + 1) * dg)
                h = hn_ref[pl.ds(t * nseq + n0, ROWS), cols]
                s = h
                for j in range(1, w):
                    i = lp + t - j
                    if i < lp:
                        s = s + hp_refs[i][pl.ds(n0, ROWS), cols]
                    else:
                        s = s + hn_ref[pl.ds((i - lp) * nseq + n0, ROWS), cols]
                cnt = float(min(pos0 + t + 1, w))
                p_ref[pl.ds(t * nseq + n0, ROWS), cols] = (s / cnt - h).astype(BF16)
        return carry

    lax.fori_loop(0, nseq // ROWS, chunk, 0)
    _pool_project(x_ref, p_ref, w_ref, sc_ref, o_ref)


def _poolmix_sample(x, g, pw, scale, state_p, o, layer, nseq):
    m, d = x.shape
    blk = pl.BlockSpec((m, d), lambda i: (0, 0))
    ops_p, specs_p = _history_blocks(state_p, o)
    return pl.pallas_call(
        functools.partial(_poolmix_sample_kernel, nseq=nseq, nt=m // nseq, pos0=PAST_LEN),
        grid=(1,),
        in_specs=[blk, _layer_spec(g, layer, 1), _layer_spec(pw, o, 1), _layer_spec(scale, o, 1)] + specs_p,
        out_specs=[blk, blk],
        out_shape=[jax.ShapeDtypeStruct((m, d), F32), jax.ShapeDtypeStruct((m, d), F32)],
        scratch_shapes=[pltpu.VMEM((m, d), BF16)],
        compiler_params=_params("arbitrary"),
        name="poolmix_sample",
    )(x, g, pw, scale, *ops_p)


def _mlp_kernel(x_ref, xr_ref, g_ref, wu_ref, wd_ref, o_ref, xn_ref, h_ref, *, nf):
    s = pl.program_id(1)
    tf = h_ref.shape[2]

    @pl.when(s == 0)
    def _():
        xn_ref[...] = _rms(x_ref[...], g_ref[...]).astype(BF16)

    @pl.when(s < nf)
    def _():
        up = jnp.dot(xn_ref[...], wu_ref[...], preferred_element_type=F32)
        h_ref[s] = jnp.square(jnp.maximum(up, 0.0)).astype(BF16)

    @pl.when(s >= nf)
    def _():
        acc = xr_ref[...]
        for c in range(nf):
            acc = acc + jnp.dot(h_ref[c], wd_ref[c * tf:(c + 1) * tf, :], preferred_element_type=F32)
        o_ref[...] = acc


def _mlp(x, g, wu, wd, layer, tm, tf=1024, tn=256):
    m, d = x.shape
    ff = wu.shape[1]
    nf, nn = ff // tf, d // tn
    col = lambda i, s: (i, jnp.maximum(s - nf, 0))
    return pl.pallas_call(
        functools.partial(_mlp_kernel, nf=nf),
        grid=(m // tm, nf + nn),
        in_specs=[pl.BlockSpec((tm, d), lambda i, s: (i, 0)),
                  pl.BlockSpec((tm, tn), col),
                  _layer_spec(g, layer, 2),
                  pl.BlockSpec((d, tf), lambda i, s: (0, jnp.minimum(s, nf - 1))),
                  pl.BlockSpec((ff, tn), lambda i, s: (0, jnp.maximum(s - nf, 0)))],
        out_specs=pl.BlockSpec((tm, tn), col),
        out_shape=jax.ShapeDtypeStruct((m, d), F32),
        scratch_shapes=[pltpu.VMEM((tm, d), BF16), pltpu.VMEM((nf, tm, tf), BF16)],
        compiler_params=_params("parallel", "arbitrary"),
        name="mlp",
    )(x, x, g, wu, wd)


def _proj_mlp_kernel(xc_ref, u_ref, wo_ref, g_ref, wu_ref, wd_ref, o_ref, x1_ref, xn_ref, h_ref,
                     *, no, nf):
    s = pl.program_id(1)
    nn, _, tn = x1_ref.shape
    tf = h_ref.shape[2]
    per = xc_ref.shape[1] // tn

    @pl.when(s < no)
    def _():
        y = xc_ref[...] + jnp.dot(u_ref[...], wo_ref[...], preferred_element_type=F32)
        for c in range(per):
            x1_ref[s * per + c] = y[:, c * tn:(c + 1) * tn]

    @pl.when(s == no)
    def _():
        ss = None
        for c in range(nn):
            xc = x1_ref[c]
            part = jnp.sum(xc * xc, axis=-1, keepdims=True)
            ss = part if ss is None else ss + part
        inv = lax.rsqrt(ss / (nn * tn) + EPS)
        for c in range(nn):
            cols = slice(c * tn, (c + 1) * tn)
            xn_ref[:, cols] = (x1_ref[c] * inv * g_ref[:, cols]).astype(BF16)

    @pl.when((s >= no) & (s < no + nf))
    def _():
        up = jnp.dot(xn_ref[...], wu_ref[...], preferred_element_type=F32)
        h_ref[s - no] = jnp.square(jnp.maximum(up, 0.0)).astype(BF16)

    @pl.when(s >= no + nf)
    def _():
        acc = x1_ref[s - no - nf]
        for c in range(nf):
            acc = acc + jnp.dot(h_ref[c], wd_ref[c * tf:(c + 1) * tf, :], preferred_element_type=F32)
        o_ref[...] = acc


def _proj_mlp(x, u, wo, g, wu, wd, layer, tm, tno=512, tf=1024, tn=256):
    m, d = x.shape
    ku, ff = u.shape[1], wu.shape[1]
    no, nf, nn = d // tno, ff // tf, d // tn
    clip = lambda v, hi: jnp.clip(v, 0, hi)
    return pl.pallas_call(
        functools.partial(_proj_mlp_kernel, no=no, nf=nf),
        grid=(m // tm, no + nf + nn),
        in_specs=[pl.BlockSpec((tm, tno), lambda i, s: (i, clip(s, no - 1))),
                  pl.BlockSpec((tm, ku), lambda i, s: (i, 0)),
                  pl.BlockSpec((ku, tno), lambda i, s: (0, clip(s, no - 1))),
                  _layer_spec(g, layer, 2),
                  pl.BlockSpec((d, tf), lambda i, s: (0, clip(s - no, nf - 1))),
                  pl.BlockSpec((ff, tn), lambda i, s: (0, clip(s - no - nf, nn - 1)))],
        out_specs=pl.BlockSpec((tm, tn), lambda i, s: (i, clip(s - no - nf, nn - 1))),
        out_shape=jax.ShapeDtypeStruct((m, d), F32),
        scratch_shapes=[pltpu.VMEM((nn, tm, tn), F32), pltpu.VMEM((tm, d), BF16),
                        pltpu.VMEM((nf, tm, tf), BF16)],
        compiler_params=_params("parallel", "arbitrary"),
        name="proj_mlp",
    )(x, u, wo, g, wu, wd)


def _mlp_cast_kernel(x_ref, g_ref, wu_ref, wd_ref, o_ref, wub_ref, wdb_ref, xn_ref):
    @pl.when(pl.program_id(0) == 0)
    def _():
        x = x_ref[...]
        xn_ref[...] = _rms(x, g_ref[...]).astype(BF16)
        o_ref[...] = x

    wub = wu_ref[...].astype(BF16)
    wdb = wd_ref[...].astype(BF16)
    wub_ref[...] = wub
    wdb_ref[...] = wdb
    up = jnp.dot(xn_ref[...], wub, preferred_element_type=F32)
    h = jnp.square(jnp.maximum(up, 0.0)).astype(BF16)
    o_ref[...] += jnp.dot(h, wdb, preferred_element_type=F32)


def _mlp_cast(x, g, wu, wd, layer, tf=512):
    m, d = x.shape
    ff = wu.shape[2]
    row = pl.BlockSpec((m, d), lambda f: (0, 0))
    return pl.pallas_call(
        _mlp_cast_kernel,
        grid=(ff // tf,),
        in_specs=[row, _layer_spec(g, layer, 1),
                  pl.BlockSpec((None, d, tf), lambda f: (layer, 0, f)),
                  pl.BlockSpec((None, tf, d), lambda f: (layer, f, 0))],
        out_specs=[row, pl.BlockSpec((d, tf), lambda f: (0, f)), pl.BlockSpec((tf, d), lambda f: (f, 0))],
        out_shape=[jax.ShapeDtypeStruct((m, d), F32), jax.ShapeDtypeStruct((d, ff), BF16),
                   jax.ShapeDtypeStruct((ff, d), BF16)],
        scratch_shapes=[pltpu.VMEM((m, d), BF16)],
        compiler_params=_params("arbitrary"),
        name="mlp_cast",
    )(x, g, wu, wd)


def _final_kernel(x_ref, g_ref, o_ref):
    o_ref[...] = _rms(x_ref[...], g_ref[...])


def _final_prompt(x, g, nseq, n_meta, ts=512):
    m, d = x.shape
    t = m // nseq
    seq = t - n_meta
    return pl.pallas_call(
        _final_kernel,
        grid=(nseq, seq // ts),
        in_specs=[pl.BlockSpec((pl.Element(1), pl.Element(ts), pl.Element(d)),
                               lambda b, j: (b, pl.multiple_of(n_meta + j * ts, SUBLANES), 0)),
                  pl.BlockSpec((1, d), lambda b, j: (0, 0))],
        out_specs=pl.BlockSpec((1, ts, d), lambda b, j: (b, j, 0)),
        out_shape=jax.ShapeDtypeStruct((nseq, seq, d), F32),
        compiler_params=_params("parallel", "parallel"),
        name="final_prompt",
    )(x.reshape(nseq, t, d), g)


def _final_sample(x, g, nseq):
    m, d = x.shape
    nt = m // nseq
    y = pl.pallas_call(
        _final_kernel,
        grid=(nt,),
        in_specs=[pl.BlockSpec((nseq, d), lambda t: (t, 0)), pl.BlockSpec((1, d), lambda t: (0, 0))],
        out_specs=pl.BlockSpec((nseq, d), lambda t: (0, t)),
        out_shape=jax.ShapeDtypeStruct((nseq, nt * d), F32),
        compiler_params=_params("parallel"),
        name="final_sample",
    )(x, g)
    return y.reshape(nseq, nt, d)


def _trunks(xp, xs, nseq_p, nseq_s, tm, tt, states, p):
    depth = p["norm_mix"].shape[0]
    conv = (p["conv_a_w"], p["conv_a_b"], p["ln_a_g"], p["ln_a_b"], p["conv_b_w"])
    new = {k: [] for k in ("a_p", "b_p", "p_p", "a_s", "b_s", "p_s")}
    tmaj = lambda z: z.reshape(-1, nseq_s, z.shape[-1]).transpose(1, 0, 2)
    tail = lambda state, rows: jnp.concatenate([state, tmaj(rows)], axis=1)[:, -state.shape[1]:]
    for layer in range(depth):
        if layer % 2 == 0:
            e = layer // 2
            (a, v, bg), w_groups = _inproj_cast(xs, p["norm_mix"], p["w_in_even"], e, layer)
            u = _evenmix_sample(a, v, bg, states[0], states[1], conv, e, nseq=nseq_s)
            new["a_s"].append(tail(states[0][e], a))
            new["b_s"].append(tail(states[1][e], v))
            xs, wo = _outproj_cast(xs, u, p["w_out_even"], e)
            xs, wu, wd = _mlp_cast(xs, p["norm_mlp"], p["w_mlp_up"], p["w_mlp_down"], layer)

            a, v, bg = _inproj(xp, p["norm_mix"], w_groups, layer, tm)
            u, a_tail, v_tail = _evenmix_prompt(a, v, bg, conv, e, nseq=nseq_p, tt=tt)
            new["a_p"].append(a_tail[:, -(conv[0].shape[1] - 1):])
            new["b_p"].append(v_tail[:, -(conv[4].shape[1] - 1):])
            xp = _proj_mlp(xp, u, wo, p["norm_mlp"], wu, wd, layer, tm)
        else:
            o = layer // 2
            xs, hn = _poolmix_sample(xs, p["norm_mix"], p["pool_w"], p["pool_scale"], states[2], o,
                                     layer, nseq=nseq_s)
            new["p_s"].append(tail(states[2][o], hn))
            xs, wu, wd = _mlp_cast(xs, p["norm_mlp"], p["w_mlp_up"], p["w_mlp_down"], layer)

            xp, h_tail = _poolmix_prompt(xp, p["norm_mix"], p["pool_w"], p["pool_scale"], o, layer,
                                         nseq=nseq_p, tt=tt)
            new["p_p"].append(h_tail[:, -(POOL_WINDOWS[-1] - 1):])
            xp = _mlp(xp, p["norm_mlp"], wu, wd, layer, tm)
    return xp, xs, {k: jnp.stack(v) for k, v in new.items()}


def kernel(x_prompt, x_sample, state_conv_a, state_conv_b, state_pool, meta_tokens, norm_mix, norm_mlp, norm_final, w_in_even, conv_a_w, conv_a_b, ln_a_g, ln_a_b, conv_b_w, w_out_even, pool_w, pool_scale, w_mlp_up, w_mlp_down):
    vec = lambda z: z.reshape(z.shape[0], 1, z.shape[1])
    p = dict(norm_mix=vec(norm_mix), norm_mlp=vec(norm_mlp), w_in_even=w_in_even,
             conv_a_w=conv_a_w, conv_a_b=vec(conv_a_b), ln_a_g=vec(ln_a_g), ln_a_b=vec(ln_a_b),
             conv_b_w=conv_b_w, w_out_even=w_out_even, pool_w=pool_w.astype(BF16),
             pool_scale=vec(pool_scale), w_mlp_up=w_mlp_up, w_mlp_down=w_mlp_down)
    gf = norm_final.reshape(1, -1)
    batch, seq, d = x_prompt.shape
    n_meta = meta_tokens.shape[0]
    t = n_meta + seq
    meta = jnp.broadcast_to(meta_tokens.astype(x_prompt.dtype)[None], (batch, n_meta, d))
    xp = jnp.concatenate([meta, x_prompt], axis=1).reshape(batch * t, d)
    nb, nt, _ = x_sample.shape
    xs = x_sample.transpose(1, 0, 2).reshape(nt * nb, d)
    tt = t // 3
    yp, ys, new = _trunks(xp, xs, batch, nb, tt, tt, (state_conv_a, state_conv_b, state_pool), p)
    y_prompt = _final_prompt(yp, gf, batch, n_meta)
    y_sample = _final_sample(ys, gf, nb)
    return (y_prompt, y_sample, new["a_p"], new["b_p"], new["p_p"], new["a_s"], new["b_s"], new["p_s"])
```
